```python
import jax, jax.numpy as jnp
from jax import lax
import numpy as np

D_MODEL = 1024
BATCH = 8
SEQ = 8192
DEPTH = 4

CTX_LEN = 256
GRID_W = 64
N_MIXERS = 3
N_LAYERS_MLA = (DEPTH + 2) // 3
N_LAYERS_SSD = (DEPTH + 1) // 3
N_LAYERS_FNET = DEPTH // 3
N_MOD = 6
NORM_EPS = 1e-6

MLA_HEADS = 8
MLA_Q_LORA = 256
MLA_KV_LORA = 128
MLA_NOPE = 128
MLA_ROPE = 64
MLA_QK_DIM = MLA_NOPE + MLA_ROPE
MLA_V = 128
MLA_IN_DIM = MLA_Q_LORA + MLA_KV_LORA + MLA_ROPE
ROPE_THETA = 10000.0
ROPE_HALF = MLA_ROPE // 2
ROPE_AXIS_FREQS = ROPE_HALF // 2
Q_BLOCK = 128

SSD_INNER = 2 * D_MODEL
SSD_HEADDIM = 64
SSD_HEADS = SSD_INNER // SSD_HEADDIM
SSD_GROUPS = 4
SSD_HEADS_PER_GROUP = SSD_HEADS // SSD_GROUPS
SSD_STATE = 128
SSD_CONV = 3
SSD_CHUNK = 128
SSD_GN = SSD_GROUPS * SSD_STATE
SSD_CONV_DIM = SSD_INNER + 2 * SSD_GN
SSD_IN_DIM = SSD_INNER + SSD_CONV_DIM + 2 * SSD_HEADS

FNET_GROUPS = 4
FNET_GROUP_DIM = D_MODEL // FNET_GROUPS

N_EXPERTS = 16
EXPERT_FF = 1024
CAPACITY_FACTOR = 2

kernel_name = "hybrid_mla_ssd_fnet_ecmoe_diffusion"


def rmsnorm(x, g):
    xf = x.astype(jnp.float32)
    y = xf * lax.rsqrt(jnp.mean(xf * xf, axis=-1, keepdims=True) + NORM_EPS)
    return (y * g.astype(jnp.float32)).astype(x.dtype)


def modulate(h, shift, scale):
    return h * (1 + scale) + shift


def rope_2d_tables(n_tokens):
    rows_n = n_tokens // GRID_W
    row = jnp.repeat(jnp.arange(rows_n, dtype=jnp.float32), GRID_W)
    col = jnp.tile(jnp.arange(GRID_W, dtype=jnp.float32), rows_n)
    inv = ROPE_THETA ** (-jnp.arange(ROPE_AXIS_FREQS, dtype=jnp.float32) / ROPE_AXIS_FREQS)
    ang = jnp.concatenate([row[:, None] * inv, col[:, None] * inv], axis=-1)
    return jnp.cos(ang), jnp.sin(ang)


def apply_rope_tail(t, cos, sin):
    cs = cos[None, :, None, :].astype(t.dtype)
    sn = sin[None, :, None, :].astype(t.dtype)
    nope, r = t[..., :MLA_NOPE], t[..., MLA_NOPE:]
    r1, r2 = r[..., :ROPE_HALF], r[..., ROPE_HALF:]
    return jnp.concatenate([nope, r1 * cs - r2 * sn, r2 * cs + r1 * sn], axis=-1)


def mla_project(h, w_in, g_q, w_uq, g_kv, w_ukv, g_qn, g_kn):
    b, t, _ = h.shape
    a = h @ w_in
    q_a, kv_a, k_r = a[..., :MLA_Q_LORA], a[..., MLA_Q_LORA:MLA_Q_LORA + MLA_KV_LORA], a[..., MLA_Q_LORA + MLA_KV_LORA:]
    q = (rmsnorm(q_a, g_q) @ w_uq).reshape(b, t, MLA_HEADS, MLA_QK_DIM)
    kv = (rmsnorm(kv_a, g_kv) @ w_ukv).reshape(b, t, MLA_HEADS, MLA_NOPE + MLA_V)
    k_nope, v = kv[..., :MLA_NOPE], kv[..., MLA_NOPE:]
    k = jnp.concatenate([k_nope, jnp.broadcast_to(k_r[:, :, None, :], (b, t, MLA_HEADS, MLA_ROPE))], axis=-1)
    return rmsnorm(q, g_qn), rmsnorm(k, g_kn), v


def attend(q, k, v):
    s = jnp.einsum('bqhd,bkhd->bhqk', q, k).astype(jnp.float32) * (MLA_QK_DIM ** -0.5)
    p = jax.nn.softmax(s, axis=-1).astype(v.dtype)
    return jnp.einsum('bhqk,bkhd->bqhd', p, v)


def mla_mixer(hl, hc, w_in, g_q, w_uq, g_kv, w_ukv, g_qn, g_kn, w_o, need_ctx):
    b, t, _ = hl.shape
    cos, sin = rope_2d_tables(t)
    ql, kl, vl = mla_project(hl, w_in, g_q, w_uq, g_kv, w_ukv, g_qn, g_kn)
    ql, kl = apply_rope_tail(ql, cos, sin), apply_rope_tail(kl, cos, sin)
    qc, kc, vc = mla_project(hc, w_in, g_q, w_uq, g_kv, w_ukv, g_qn, g_kn)
    k_all = jnp.concatenate([kc, kl], axis=1)
    v_all = jnp.concatenate([vc, vl], axis=1)
    nb = t // Q_BLOCK
    qb = ql.reshape(b, nb, Q_BLOCK, MLA_HEADS, MLA_QK_DIM).swapaxes(0, 1)
    ob = lax.map(lambda blk: attend(blk, k_all, v_all), qb)
    out_l = ob.swapaxes(0, 1).reshape(b, t, MLA_HEADS * MLA_V) @ w_o
    out_c = None
    if need_ctx:
        out_c = attend(qc, kc, vc).reshape(b, hc.shape[1], MLA_HEADS * MLA_V) @ w_o
    return out_l, out_c


def depthwise_conv_centred(x, w, bias):
    ch = x.shape[-1]
    pad = (SSD_CONV - 1) // 2
    y = lax.conv_general_dilated(x, w[:, None, :].astype(x.dtype), window_strides=(1,),
                                 padding=[(pad, pad)], dimension_numbers=('NWC', 'WIO', 'NWC'),
                                 feature_group_count=ch)
    return y + bias.astype(x.dtype)


def ssd_in(h, w_in, conv_w, conv_b, dt_bias):
    b, t, _ = h.shape
    zxbcdt = h @ w_in
    z = zxbcdt[..., :SSD_INNER]
    xbc = jax.nn.silu(depthwise_conv_centred(zxbcdt[..., SSD_INNER:SSD_INNER + SSD_CONV_DIM], conv_w, conv_b))
    dt_raw = zxbcdt[..., SSD_INNER + SSD_CONV_DIM:]
    xs = xbc[..., :SSD_INNER].reshape(b, t, SSD_HEADS, SSD_HEADDIM)
    bm = xbc[..., SSD_INNER:SSD_INNER + SSD_GN].reshape(b, t, SSD_GROUPS, SSD_STATE)
    cm = xbc[..., SSD_INNER + SSD_GN:].reshape(b, t, SSD_GROUPS, SSD_STATE)
    dt = jax.nn.softplus(dt_raw.astype(jnp.float32) + dt_bias.reshape(-1).astype(jnp.float32)).reshape(b, t, 2, SSD_HEADS)
    return z, xs, bm, cm, dt[:, :, 0], dt[:, :, 1]


def ssd_scan(xs, dt, a_dir, bm, cm, init_state):
    b, t = xs.shape[:2]
    nc = t // SSD_CHUNK

    def to_chunks(u):
        return u.reshape((b, nc, SSD_CHUNK) + u.shape[2:]).swapaxes(0, 1)

    mask = jnp.tril(jnp.ones((SSD_CHUNK, SSD_CHUNK), dtype=bool))[None, :, :, None]

    def body(state, inp):
        xc, dtc, bc, cc = inp
        acum = jnp.cumsum(dtc * a_dir, axis=1)
        diff = acum[:, :, None, :] - acum[:, None, :, :]
        lmat = jnp.exp(jnp.where(mask, diff, -jnp.inf))
        cb = jnp.repeat(jnp.einsum('bign,bjgn->bijg', cc, bc), SSD_HEADS_PER_GROUP, axis=-1)
        xdt = xc * dtc[..., None]
        y_diag = jnp.einsum('bijh,bjhp->bihp', cb * lmat, xdt)
        ch = jnp.repeat(cc, SSD_HEADS_PER_GROUP, axis=2)
        bh = jnp.repeat(bc, SSD_HEADS_PER_GROUP, axis=2)
        y_off = jnp.einsum('bihn,bhpn->bihp', ch, state) * jnp.exp(acum)[..., None]
        decay = jnp.exp(acum[:, -1:, :] - acum)
        new_state = (state * jnp.exp(acum[:, -1, :])[:, :, None, None]
                     + jnp.einsum('bjhn,bjhp->bhpn', bh * decay[..., None], xdt))
        return new_state, y_diag + y_off

    f32 = jnp.float32
    inputs = (to_chunks(xs.astype(f32)), to_chunks(dt), to_chunks(bm.astype(f32)), to_chunks(cm.astype(f32)))
    final, ys = lax.scan(body, init_state, inputs)
    return ys.swapaxes(0, 1).reshape(b, t, SSD_HEADS, SSD_HEADDIM), final


def ssd_bidir(xs, bm, cm, dt_f, dt_b, a, init_f, init_b):
    flip = lambda u: jnp.flip(u, axis=1)
    y_f, s_f = ssd_scan(xs, dt_f, a[0], bm, cm, init_f)
    y_b, s_b = ssd_scan(flip(xs), flip(dt_b), a[1], flip(bm), flip(cm), init_b)
    return y_f + flip(y_b), s_f, s_b


def ssd_out(y, xs, z, d_skip, g_norm, w_out):
    b, t = xs.shape[:2]
    y = (y + xs.astype(jnp.float32) * d_skip.astype(jnp.float32)[:, None]).astype(xs.dtype)
    gated = (y.reshape(b, t, SSD_INNER) * jax.nn.silu(z)).reshape(b, t, SSD_GROUPS, SSD_INNER // SSD_GROUPS)
    normed = rmsnorm(gated, g_norm.reshape(SSD_GROUPS, SSD_INNER // SSD_GROUPS)).reshape(b, t, SSD_INNER)
    return normed @ w_out


def ssd_mixer(hl, hc, w_in, conv_w, conv_b, dt_bias, a_log, d_skip, g_norm, w_out, need_ctx):
    b = hl.shape[0]
    a = -jnp.exp(a_log.astype(jnp.float32))
    zero = jnp.zeros((b, SSD_HEADS, SSD_HEADDIM, SSD_STATE), jnp.float32)
    zc, xc, bc, cc, dcf, dcb = ssd_in(hc, w_in, conv_w, conv_b, dt_bias)
    yc, s_f, s_b = ssd_bidir(xc, bc, cc, dcf, dcb, a, zero, zero)
    zl, xl, bl, cl, dlf, dlb = ssd_in(hl, w_in, conv_w, conv_b, dt_bias)
    yl, _, _ = ssd_bidir(xl, bl, cl, dlf, dlb, a, s_f, s_b)
    out_l = ssd_out(yl, xl, zl, d_skip, g_norm, w_out)
    out_c = ssd_out(yc, xc, zc, d_skip, g_norm, w_out) if need_ctx else None
    return out_l, out_c


def fnet_mixer(h, w_o):
    b, t, d = h.shape
    hg = h.astype(jnp.float32).reshape(b, t, FNET_GROUPS, FNET_GROUP_DIM)
    mixed = jnp.fft.fft2(hg, axes=(1, 3), norm='ortho').real.astype(h.dtype)
    return mixed.reshape(b, t, d) @ w_o


def ec_moe(h, w_router, w_gate, w_up, w_down):
    b, t, d = h.shape
    cap = CAPACITY_FACTOR * t // N_EXPERTS
    aff = jax.nn.softmax(jnp.einsum('btd,de->bte', h, w_router).astype(jnp.float32), axis=-1)
    gate, idx = lax.top_k(jnp.swapaxes(aff, 1, 2), cap)
    xe = jax.vmap(lambda hb, ib: hb[ib])(h, idx)
    hid = jax.nn.silu(jnp.einsum('becd,edf->becf', xe, w_gate)) * jnp.einsum('becd,edf->becf', xe, w_up)
    ye = jnp.einsum('becf,efd->becd', hid, w_down) * gate[..., None].astype(h.dtype)
    return jax.vmap(lambda ib, yb: jnp.zeros((t, d), yb.dtype).at[ib.reshape(-1)].add(yb.reshape(-1, d)))(idx, ye)


def setup_inputs(seed: int = 0) -> dict:
    key = jax.random.key(seed)
    ks = iter(jax.random.split(key, 40))
    f32 = jnp.float32
    nrm = lambda shape, fan_in: jax.random.normal(next(ks), shape, f32) * (fan_in ** -0.5)
    gain = lambda shape: 1.0 + 0.02 * jax.random.normal(next(ks), shape, f32)
    small = lambda shape: 0.02 * jax.random.normal(next(ks), shape, f32)
    D = D_MODEL
    x = jax.random.normal(next(ks), (BATCH, SEQ, D), f32)
    c = jax.random.normal(next(ks), (BATCH, D), f32)
    ctx = jax.random.normal(next(ks), (BATCH, CTX_LEN, D), f32)
    c_ctx = jax.random.normal(next(ks), (D,), f32)
    w_mod = 0.5 * nrm((DEPTH, D, N_MOD * D), D)
    b_mod = small((DEPTH, N_MOD * D))
    g_mix = gain((DEPTH, D))
    g_ffn = gain((DEPTH, D))
    nA, nB, nC = N_LAYERS_MLA, N_LAYERS_SSD, N_LAYERS_FNET
    mla_w_in = nrm((nA, D, MLA_IN_DIM), D)
    mla_g_q = gain((nA, MLA_Q_LORA))
    mla_w_uq = nrm((nA, MLA_Q_LORA, MLA_HEADS * MLA_QK_DIM), MLA_Q_LORA)
    mla_g_kv = gain((nA, MLA_KV_LORA))
    mla_w_ukv = nrm((nA, MLA_KV_LORA, MLA_HEADS * (MLA_NOPE + MLA_V)), MLA_KV_LORA)
    mla_g_qn = gain((nA, MLA_QK_DIM))
    mla_g_kn = gain((nA, MLA_QK_DIM))
    mla_w_o = nrm((nA, MLA_HEADS * MLA_V, D), MLA_HEADS * MLA_V)
    ssd_w_in = nrm((nB, D, SSD_IN_DIM), D)
    ssd_conv_w = nrm((nB, SSD_CONV, SSD_CONV_DIM), SSD_CONV)
    ssd_conv_b = small((nB, SSD_CONV_DIM))
    dt0 = jnp.exp(jax.random.uniform(next(ks), (nB, 2, SSD_HEADS), f32, np.log(1e-3), np.log(1e-1)))
    ssd_dt_bias = dt0 + jnp.log(-jnp.expm1(-dt0))
    ssd_a_log = jnp.log(jax.random.uniform(next(ks), (nB, 2, SSD_HEADS), f32, 1.0, 16.0))
    ssd_d = gain((nB, SSD_HEADS))
    ssd_g_norm = gain((nB, SSD_INNER))
    ssd_w_out = nrm((nB, SSD_INNER, D), SSD_INNER)
    fnet_w_o = nrm((nC, D, D), D)
    moe_w_router = nrm((DEPTH, D, N_EXPERTS), D)
    moe_w_gate = nrm((DEPTH, N_EXPERTS, D, EXPERT_FF), D)
    moe_w_up = nrm((DEPTH, N_EXPERTS, D, EXPERT_FF), D)
    moe_w_down = nrm((DEPTH, N_EXPERTS, EXPERT_FF, D), EXPERT_FF)
    return {"x": x, "c": c, "ctx": ctx, "c_ctx": c_ctx, "w_mod": w_mod, "b_mod": b_mod,
            "g_mix": g_mix, "g_ffn": g_ffn,
            "mla_w_in": mla_w_in, "mla_g_q": mla_g_q, "mla_w_uq": mla_w_uq, "mla_g_kv": mla_g_kv,
            "mla_w_ukv": mla_w_ukv, "mla_g_qn": mla_g_qn, "mla_g_kn": mla_g_kn, "mla_w_o": mla_w_o,
            "ssd_w_in": ssd_w_in, "ssd_conv_w": ssd_conv_w, "ssd_conv_b": ssd_conv_b,
            "ssd_dt_bias": ssd_dt_bias, "ssd_a_log": ssd_a_log, "ssd_d": ssd_d,
            "ssd_g_norm": ssd_g_norm, "ssd_w_out": ssd_w_out, "fnet_w_o": fnet_w_o,
            "moe_w_router": moe_w_router, "moe_w_gate": moe_w_gate, "moe_w_up": moe_w_up,
            "moe_w_down": moe_w_down}


def reference(x, c, ctx, c_ctx, w_mod, b_mod, g_mix, g_ffn,
              mla_w_in, mla_g_q, mla_w_uq, mla_g_kv, mla_w_ukv, mla_g_qn, mla_g_kn, mla_w_o,
              ssd_w_in, ssd_conv_w, ssd_conv_b, ssd_dt_bias, ssd_a_log, ssd_d, ssd_g_norm, ssd_w_out,
              fnet_w_o, moe_w_router, moe_w_gate, moe_w_up, moe_w_down):
    cx = ctx
    silu_c = jax.nn.silu(c)
    silu_cc = jax.nn.silu(c_ctx)
    for i in range(DEPTH):
        kind, j = i % N_MIXERS, i // N_MIXERS
        last = i == DEPTH - 1
        mod_l = (silu_c @ w_mod[i] + b_mod[i])[:, None, :]
        mod_c = (silu_cc @ w_mod[i] + b_mod[i])[None, None, :]
        sh_a, sc_a, gt_a, sh_f, sc_f, gt_f = jnp.split(mod_l, N_MOD, axis=-1)
        csh_a, csc_a, cgt_a, csh_f, csc_f, cgt_f = jnp.split(mod_c, N_MOD, axis=-1)
        hl = modulate(rmsnorm(x, g_mix[i]), sh_a, sc_a)
        if kind == 0:
            hc = modulate(rmsnorm(cx, g_mix[i]), csh_a, csc_a)
            ol, oc = mla_mixer(hl, hc, mla_w_in[j], mla_g_q[j], mla_w_uq[j], mla_g_kv[j], mla_w_ukv[j],
                               mla_g_qn[j], mla_g_kn[j], mla_w_o[j], not last)
        elif kind == 1:
            hc = modulate(rmsnorm(cx, g_mix[i]), csh_a, csc_a)
            ol, oc = ssd_mixer(hl, hc, ssd_w_in[j], ssd_conv_w[j], ssd_conv_b[j], ssd_dt_bias[j],
                               ssd_a_log[j], ssd_d[j], ssd_g_norm[j], ssd_w_out[j], not last)
        else:
            ol = fnet_mixer(hl, fnet_w_o[j])
            oc = fnet_mixer(modulate(rmsnorm(cx, g_mix[i]), csh_a, csc_a), fnet_w_o[j]) if not last else None
        x = x + gt_a * ol
        x = x + gt_f * ec_moe(modulate(rmsnorm(x, g_ffn[i]), sh_f, sc_f),
                              moe_w_router[i], moe_w_gate[i], moe_w_up[i], moe_w_down[i])
        if not last:
            cx = cx + cgt_a * oc
            cx = cx + cgt_f * ec_moe(modulate(rmsnorm(cx, g_ffn[i]), csh_f, csc_f),
                                     moe_w_router[i], moe_w_gate[i], moe_w_up[i], moe_w_down[i])
    return x
```

```python
import functools
import math

import jax
import jax.numpy as jnp
from jax import lax
from jax.experimental import pallas as pl
from jax.experimental.pallas import tpu as pltpu

F32 = jnp.float32
BF16 = jnp.bfloat16

NORM_EPS = 1e-6
GRID_W = 64
N_MOD = 6
ROPE_THETA = 10000.0

MLA_HEADS = 8
MLA_Q_LORA = 256
MLA_KV_LORA = 128
MLA_NOPE = 128
MLA_ROPE = 64
MLA_QK_DIM = MLA_NOPE + MLA_ROPE
MLA_V = 128
MLA_DK_PAD = 256

SSD_HEADDIM = 64
SSD_HEADS = 32
SSD_GROUPS = 4
SSD_HPG = SSD_HEADS // SSD_GROUPS
SSD_STATE = 128
SSD_CHUNK = 128
SSD_INNER = SSD_HEADS * SSD_HEADDIM
SSD_GN = SSD_GROUPS * SSD_STATE
SSD_CONV_DIM = SSD_INNER + 2 * SSD_GN
SSD_GROUP_W = SSD_INNER // SSD_GROUPS

FNET_GROUPS = 4
N_EXPERTS = 16
CAPACITY_FACTOR = 2

VMEM_LIMIT_BYTES = 52 * 1024 * 1024
HALO = 8


def _cparams(*sem):
    return pltpu.CompilerParams(dimension_semantics=sem, vmem_limit_bytes=VMEM_LIMIT_BYTES)


def _dot(a, b):
    return jnp.dot(a, b, preferred_element_type=F32)


def _dot_nt(a, b):
    return lax.dot_general(a, b, (((1,), (1,)), ((), ())), preferred_element_type=F32)


def _dot_tn(a, b):
    return lax.dot_general(a, b, (((0,), (0,)), ((), ())), preferred_element_type=F32)


def _split2(x):
    hi = x.astype(BF16)
    lo = (x - hi.astype(F32)).astype(BF16)
    return hi, lo


def _split3(x):
    x1 = x.astype(BF16)
    r = x - x1.astype(F32)
    x2 = r.astype(BF16)
    x3 = (r - x2.astype(F32)).astype(BF16)
    return x1, x2, x3


def _rms(x):
    return x * lax.rsqrt(jnp.mean(x * x, axis=-1, keepdims=True) + NORM_EPS)


def _prenorm(x, g, shift, scale):
    return (_rms(x) * g) * (1.0 + scale) + shift


def _silu(x):
    return x * jax.nn.sigmoid(x)


def _softplus(x):
    return jnp.maximum(x, 0.0) + jnp.log1p(jnp.exp(-jnp.abs(x)))


def _mod_kernel(cc_ref, w_ref, b_ref, o_ref):
    s = _silu(cc_ref[...])
    s_hi, s_lo = _split2(s)
    w_hi, w_lo = _split2(w_ref[0])
    acc = _dot(s_hi, w_hi) + _dot(s_lo, w_hi) + _dot(s_hi, w_lo)
    o_ref[0] = acc + b_ref[0]


def _mods(cc, w_mod, b_mod):
    depth, d, n = w_mod.shape
    rows = cc.shape[0]
    tn = 1536
    return pl.pallas_call(
        _mod_kernel,
        grid=(depth, n // tn),
        in_specs=[pl.BlockSpec((rows, d), lambda l, j: (0, 0)),
                  pl.BlockSpec((1, d, tn), lambda l, j: (l, 0, j)),
                  pl.BlockSpec((1, 1, tn), lambda l, j: (l, 0, j))],
        out_specs=pl.BlockSpec((1, rows, tn), lambda l, j: (l, 0, j)),
        out_shape=jax.ShapeDtypeStruct((depth, rows, n), F32),
        compiler_params=_cparams("parallel", "parallel"),
        name="mods",
    )(cc, w_mod, b_mod.reshape(depth, 1, n))


def _proj_res_kernel(a_ref, w_ref, x_ref, gt_ref, o_ref):
    o_ref[0] = x_ref[0] + gt_ref[0] * _dot(a_ref[0], w_ref[...])


def _proj_residual(a, w, x, gate, tm):
    b, t, d = x.shape
    din = a.shape[-1]
    return pl.pallas_call(
        _proj_res_kernel,
        grid=(b, t // tm),
        in_specs=[pl.BlockSpec((1, tm, din), lambda i, j: (i, j, 0)),
                  pl.BlockSpec((din, d), lambda i, j: (0, 0)),
                  pl.BlockSpec((1, tm, d), lambda i, j: (i, j, 0)),
                  pl.BlockSpec((1, 1, d), lambda i, j: (i, 0, 0))],
        out_specs=pl.BlockSpec((1, tm, d), lambda i, j: (i, j, 0)),
        out_shape=jax.ShapeDtypeStruct((b, t, d), F32),
        compiler_params=_cparams("parallel", "parallel"),
        name="proj_residual",
    )(a, w, x, gate)


def _mla_proj_kernel(x_ref, g_ref, sh_ref, sc_ref, win_ref, gq_ref, wuq_ref, gkv_ref, wukv_ref,
                     gqn_ref, gkn_ref, cos_ref, sin_ref, q_ref, k_ref, v_ref):
    h = _prenorm(x_ref[0], g_ref[...], sh_ref[0], sc_ref[0])
    a = _dot(h.astype(BF16), win_ref[...])
    qa = a[:, :MLA_Q_LORA]
    kva = a[:, MLA_Q_LORA:MLA_Q_LORA + MLA_KV_LORA]
    kr = a[:, 384:512]
    krs = a[:, 512:640]
    q = _dot((_rms(qa) * gq_ref[...]).astype(BF16), wuq_ref[...])
    kv = _dot((_rms(kva) * gkv_ref[...]).astype(BF16), wukv_ref[...])
    cos_t = cos_ref[...]
    sin_t = sin_ref[...]
    gqn = gqn_ref[...]
    gkn = gkn_ref[...]
    scale = MLA_QK_DIM ** -0.5
    inv_dim = 1.0 / MLA_QK_DIM
    nh = MLA_HEADS
    kr_rot = kr * gkn[1:2] * cos_t + krs * gkn[2:3] * sin_t
    kr_sq = kr * kr
    for hd in range(nh):
        qn = q[:, hd * 128:(hd + 1) * 128]
        qr = q[:, (nh + hd) * 128:(nh + hd + 1) * 128]
        qrs = q[:, (2 * nh + hd) * 128:(2 * nh + hd + 1) * 128]
        rq = lax.rsqrt(jnp.sum(qn * qn + qr * qr, axis=-1, keepdims=True) * inv_dim + NORM_EPS) * scale
        q_ref[0, hd, :, 0:128] = (qn * gqn[0:1] * rq).astype(BF16)
        q_ref[0, hd, :, 128:256] = ((qr * gqn[1:2] * cos_t + qrs * gqn[2:3] * sin_t) * rq).astype(BF16)
        kn = kv[:, hd * 256:hd * 256 + 128]
        rk = lax.rsqrt(jnp.sum(kn * kn + kr_sq, axis=-1, keepdims=True) * inv_dim + NORM_EPS)
        k_ref[0, hd, :, 0:128] = (kn * gkn[0:1] * rk).astype(BF16)
        k_ref[0, hd, :, 128:256] = (kr_rot * rk).astype(BF16)
        v_ref[0, hd] = kv[:, hd * 256 + 128:(hd + 1) * 256].astype(BF16)


def _swap_halves(w):
    half = w.shape[-1] // 2
    return jnp.concatenate([w[..., half:], w[..., :half]], axis=-1)


def _pad_last(w, n):
    return jnp.concatenate([w, jnp.zeros(w.shape[:-1] + (n - w.shape[-1],), w.dtype)], axis=-1)


def _mla_weights(w_in, w_uq, g_qn, g_kn):
    d = w_in.shape[0]
    kr = w_in[:, MLA_Q_LORA + MLA_KV_LORA:]
    w_in_ext = jnp.concatenate(
        [w_in[:, :MLA_Q_LORA + MLA_KV_LORA], _pad_last(kr, 128), _pad_last(_swap_halves(kr), 128)], axis=1)
    wq = w_uq.reshape(MLA_Q_LORA, MLA_HEADS, MLA_QK_DIM)
    nope = wq[:, :, :MLA_NOPE].reshape(MLA_Q_LORA, -1)
    rope = wq[:, :, MLA_NOPE:]
    rope_p = _pad_last(rope, 128).reshape(MLA_Q_LORA, -1)
    rope_s = _pad_last(_swap_halves(rope), 128).reshape(MLA_Q_LORA, -1)
    w_uq_ext = jnp.concatenate([nope, rope_p, rope_s], axis=1)

    def gains(g):
        return jnp.stack([g[:MLA_NOPE], _pad_last(g[MLA_NOPE:], 128), _pad_last(_swap_halves(g[MLA_NOPE:]), 128)])

    return w_in_ext.astype(BF16), w_uq_ext.astype(BF16), gains(g_qn), gains(g_kn)


def _rope_tables(t, with_position):
    if not with_position:
        cos_t = _pad_last(jnp.ones((t, MLA_ROPE), F32), 128)
        return cos_t, jnp.zeros((t, 128), F32)
    rows_n = t // GRID_W
    nfreq = MLA_ROPE // 4
    row = jnp.repeat(jnp.arange(rows_n, dtype=F32), GRID_W)
    col = jnp.tile(jnp.arange(GRID_W, dtype=F32), rows_n)
    inv = ROPE_THETA ** (-jnp.arange(nfreq, dtype=F32) / nfreq)
    ang = jnp.concatenate([row[:, None] * inv, col[:, None] * inv], axis=-1)
    cos, sin = jnp.cos(ang), jnp.sin(ang)
    return (_pad_last(jnp.concatenate([cos, cos], axis=-1), 128),
            _pad_last(jnp.concatenate([-sin, sin], axis=-1), 128))


def _mla_project(x, g_mix, shift, scale, w_in_ext, g_q, w_uq_ext, g_kv, w_ukv, gqn3, gkn3, cos_t, sin_t, tm):
    b, t, d = x.shape
    nh = MLA_HEADS
    full = lambda shape: pl.BlockSpec(shape, lambda i, j: (0,) * len(shape))
    per_b = pl.BlockSpec((1, 1, d), lambda i, j: (i, 0, 0))
    return pl.pallas_call(
        _mla_proj_kernel,
        grid=(b, t // tm),
        in_specs=[pl.BlockSpec((1, tm, d), lambda i, j: (i, j, 0)),
                  full((1, d)), per_b, per_b,
                  full(w_in_ext.shape), full((1, MLA_Q_LORA)), full(w_uq_ext.shape),
                  full((1, MLA_KV_LORA)), full(w_ukv.shape), full((3, 128)), full((3, 128)),
                  pl.BlockSpec((tm, 128), lambda i, j: (j, 0)),
                  pl.BlockSpec((tm, 128), lambda i, j: (j, 0))],
        out_specs=[pl.BlockSpec((1, nh, tm, MLA_DK_PAD), lambda i, j: (i, 0, j, 0)),
                   pl.BlockSpec((1, nh, tm, MLA_DK_PAD), lambda i, j: (i, 0, j, 0)),
                   pl.BlockSpec((1, nh, tm, MLA_V), lambda i, j: (i, 0, j, 0))],
        out_shape=[jax.ShapeDtypeStruct((b, nh, t, MLA_DK_PAD), BF16),
                   jax.ShapeDtypeStruct((b, nh, t, MLA_DK_PAD), BF16),
                   jax.ShapeDtypeStruct((b, nh, t, MLA_V), BF16)],
        compiler_params=_cparams("parallel", "parallel"),
        name="mla_project",
    )(x, g_mix.reshape(1, d), shift, scale, w_in_ext, g_q.reshape(1, -1), w_uq_ext,
      g_kv.reshape(1, -1), w_ukv, gqn3, gkn3, cos_t, sin_t)


def _flash_kernel(*refs, n_src, tk):
    q_ref = refs[0]
    o_ref = refs[1 + 2 * n_src]
    q = q_ref[0, 0]
    tq = q.shape[0]

    def step(k, v, carry):
        m, l, acc = carry
        s = _dot_nt(q, k)
        m_new = jnp.maximum(m, jnp.max(s, axis=-1, keepdims=True))
        alpha = jnp.exp(m - m_new)
        p = jnp.exp(s - m_new)
        l = alpha * l + jnp.sum(p, axis=-1, keepdims=True)
        acc = alpha * acc + _dot(p.astype(BF16), v)
        return m_new, l, acc

    carry = (jnp.full((tq, 1), -jnp.inf, F32), jnp.zeros((tq, 1), F32), jnp.zeros((tq, MLA_V), F32))
    for s_i in range(n_src):
        k_ref, v_ref = refs[1 + 2 * s_i], refs[2 + 2 * s_i]
        t_src = k_ref.shape[2]
        tks = min(tk, t_src)
        n = t_src // tks
        if n == 1:
            carry = step(k_ref[0, 0], v_ref[0, 0], carry)
        else:
            def body(j, c, k_ref=k_ref, v_ref=v_ref, tks=tks):
                off = pl.multiple_of(j * tks, tks)
                return step(k_ref[0, 0, pl.ds(off, tks), :], v_ref[0, 0, pl.ds(off, tks), :], c)
            carry = lax.fori_loop(0, n, body, carry)
    _, l, acc = carry
    o_ref[0] = (acc / l).astype(BF16)


def _flash(q, kvs, tq, tk):
    b, nh, t, dk = q.shape
    in_specs = [pl.BlockSpec((1, 1, tq, dk), lambda i, h, j: (i, h, j, 0))]
    args = [q]
    for k, v in kvs:
        ts = k.shape[2]
        in_specs.append(pl.BlockSpec((1, 1, ts, dk), lambda i, h, j: (i, h, 0, 0)))
        in_specs.append(pl.BlockSpec((1, 1, ts, MLA_V), lambda i, h, j: (i, h, 0, 0)))
        args += [k, v]
    return pl.pallas_call(
        functools.partial(_flash_kernel, n_src=len(kvs), tk=tk),
        grid=(b, nh, t // tq),
        in_specs=in_specs,
        out_specs=pl.BlockSpec((1, tq, MLA_V), lambda i, h, j: (i, j, h)),
        out_shape=jax.ShapeDtypeStruct((b, t, nh * MLA_V), BF16),
        compiler_params=_cparams("parallel", "parallel", "parallel"),
        name="mla_attention",
    )(*args)


def _mla_mixer(x, cx, i, mods_l, mods_c, g_mix, w_in, g_q, w_uq, g_kv, w_ukv, g_qn, g_kn, w_o, need_ctx):
    b, t, d = x.shape
    tc = cx.shape[1]
    w_in_ext, w_uq_ext, gqn3, gkn3 = _mla_weights(w_in, w_uq, g_qn, g_kn)
    w_ukv_b = w_ukv.astype(BF16)
    w_o_b = w_o.astype(BF16)
    cos_l, sin_l = _rope_tables(t, True)
    cos_c, sin_c = _rope_tables(tc, False)
    proj = lambda xx, mm, cs, sn, tm: _mla_project(xx, g_mix, mm[0], mm[1], w_in_ext, g_q, w_uq_ext, g_kv,
                                                   w_ukv_b, gqn3, gkn3, cs, sn, tm)
    ql, kl, vl = proj(x, mods_l, cos_l, sin_l, min(256, t))
    qc, kc, vc = proj(cx, mods_c, cos_c, sin_c, min(256, tc))
    ol = _flash(ql, [(kc, vc), (kl, vl)], min(512, t), 512)
    x_new = _proj_residual(ol, w_o_b, x, mods_l[2], min(512, t))
    cx_new = None
    if need_ctx:
        oc = _flash(qc, [(kc, vc)], min(256, tc), 512)
        cx_new = _proj_residual(oc, w_o_b, cx, mods_c[2], min(256, tc))
    return x_new, cx_new


def _ssd_in_kernel(xm_ref, xp_ref, xn_ref, g_ref, sh_ref, sc_ref, wz_ref, wxbc_ref, wdt_ref, wdtT_ref,
                   cw_ref, cb_ref, dtb_ref, dtbT_ref,
                   z_ref, xs_ref, b_ref, c_ref, dt_ref, dtT_ref, hall_ref):
    i = pl.program_id(1)
    n = pl.num_programs(1)
    tm = xm_ref.shape[1]
    g, sh, sc = g_ref[...], sh_ref[0], sc_ref[0]
    hm = _prenorm(xm_ref[0], g, sh, sc)
    hp = jnp.where(i > 0, _prenorm(xp_ref[0], g, sh, sc), 0.0)
    hn = jnp.where(i < n - 1, _prenorm(xn_ref[0], g, sh, sc), 0.0)
    hall_ref[0:HALO] = hp
    hall_ref[HALO:HALO + tm] = hm
    hall_ref[HALO + tm:2 * HALO + tm] = hn
    raw = _dot(hall_ref[...].astype(BF16), wxbc_ref[...])
    rows = tm + 2 * HALO
    up = pltpu.roll(raw, 1, 0)
    dn = pltpu.roll(raw, rows - 1, 0)
    cw = cw_ref[...]
    conv = (up * cw[0:1] + raw * cw[1:2] + dn * cw[2:3])[HALO:HALO + tm] + cb_ref[...]
    act = _silu(conv)
    xs_ref[0] = act[:, :SSD_INNER].astype(BF16)
    b_ref[0] = act[:, SSD_INNER:SSD_INNER + SSD_GN].astype(BF16)
    c_ref[0] = act[:, SSD_INNER + SSD_GN:].astype(BF16)
    hmb = hm.astype(BF16)
    z_ref[0] = _dot(hmb, wz_ref[...]).astype(BF16)
    dt_ref[0] = _softplus(_dot(hmb, wdt_ref[...]) + dtb_ref[...])
    dtT_ref[0] = _softplus(_dot_nt(wdtT_ref[...], hmb) + dtbT_ref[...])


def _ssd_in(x, g_mix, shift, scale, wz, wxbc, wdt, wdtT, conv_w, conv_b, dt_bias, tm):
    b, t, d = x.shape
    nb8 = tm // HALO
    nh2 = 2 * SSD_HEADS
    full = lambda shape: pl.BlockSpec(shape, lambda i, j: (0,) * len(shape))
    per_b = pl.BlockSpec((1, 1, d), lambda i, j: (i, 0, 0))
    tok = lambda w: pl.BlockSpec((1, tm, w), lambda i, j: (i, j, 0))
    return pl.pallas_call(
        _ssd_in_kernel,
        grid=(b, t // tm),
        in_specs=[pl.BlockSpec((1, tm, d), lambda i, j: (i, j, 0)),
                  pl.BlockSpec((1, HALO, d), lambda i, j: (i, jnp.maximum(j * nb8 - 1, 0), 0)),
                  pl.BlockSpec((1, HALO, d), lambda i, j: (i, jnp.minimum((j + 1) * nb8, t // HALO - 1), 0)),
                  full((1, d)), per_b, per_b,
                  full(wz.shape), full(wxbc.shape), full(wdt.shape), full(wdtT.shape),
                  full((3, SSD_CONV_DIM)), full((1, SSD_CONV_DIM)), full((1, nh2)), full((nh2, 1))],
        out_specs=[tok(SSD_INNER), tok(SSD_INNER), tok(SSD_GN), tok(SSD_GN), tok(nh2),
                   pl.BlockSpec((1, nh2, tm), lambda i, j: (i, 0, j))],
        out_shape=[jax.ShapeDtypeStruct((b, t, SSD_INNER), BF16),
                   jax.ShapeDtypeStruct((b, t, SSD_INNER), BF16),
                   jax.ShapeDtypeStruct((b, t, SSD_GN), BF16),
                   jax.ShapeDtypeStruct((b, t, SSD_GN), BF16),
                   jax.ShapeDtypeStruct((b, t, nh2), F32),
                   jax.ShapeDtypeStruct((b, nh2, t), F32)],
        scratch_shapes=[pltpu.VMEM((tm + 2 * HALO, d), F32)],
        compiler_params=_cparams("parallel", "parallel"),
        name="ssd_in",
    )(x, x, x, g_mix.reshape(1, d), shift, scale, wz, wxbc, wdt, wdtT, conv_w, conv_b.reshape(1, -1),
      dt_bias.reshape(1, nh2), dt_bias.reshape(nh2, 1))


def _ssd_scan_kernel(xs_ref, b_ref, c_ref, dt_ref, dtT_ref, arow_ref, acol_ref, init_ref, e_ref,
                     y_ref, fin_ref, state_ref, *, reverse):
    ci = pl.program_id(1)
    nc = pl.num_programs(1)
    q = SSD_CHUNK
    nh = SSD_HEADS

    @pl.when(ci == 0)
    def _():
        state_ref[...] = init_ref[0]

    d0 = nh if reverse else 0
    dt = dt_ref[0][:, d0:d0 + nh]
    dt_t = dtT_ref[0][d0:d0 + nh, :]
    a = dt * arow_ref[...]
    a_t = dt_t * acol_ref[...]
    ii = lax.broadcasted_iota(jnp.int32, (q, q), 0)
    jj = lax.broadcasted_iota(jnp.int32, (q, q), 1)
    keep = (jj >= ii) if reverse else (jj <= ii)
    tri = jnp.where(keep, 1.0, 0.0).astype(BF16)
    tri_t = jnp.where((ii >= jj) if reverse else (ii <= jj), 1.0, 0.0).astype(BF16)
    a1, a2, a3 = _split3(a)
    acum = _dot(tri, a1) + _dot(tri, a2) + _dot(tri, a3)
    t1, t2, t3 = _split3(a_t)
    acum_t = _dot(t1, tri_t) + _dot(t2, tri_t) + _dot(t3, tri_t)
    edge = 0 if reverse else q - 1
    e = e_ref[...]

    def expand(v):
        hi, lo = _split2(v)
        return _dot(hi, e) + _dot(lo, e)

    xdt = xs_ref[0].astype(F32) * expand(dt)
    eac_e = expand(jnp.exp(acum))
    dec_e = expand(jnp.exp(acum[edge:edge + 1, :] - acum))
    elast_e = eac_e[edge:edge + 1, :]
    xdt_b = xdt.astype(BF16)
    xdec_b = (xdt * dec_e).astype(BF16)
    lane = lax.broadcasted_iota(jnp.int32, (q, 2 * SSD_HEADDIM), 1)
    zero_b = jnp.zeros((q, 2 * SSD_HEADDIM), BF16)
    gw = SSD_GROUP_W
    for g in range(SSD_GROUPS):
        bg = b_ref[0][:, g * SSD_STATE:(g + 1) * SSD_STATE]
        cg = c_ref[0][:, g * SSD_STATE:(g + 1) * SSD_STATE]
        cb = _dot_nt(cg, bg)
        st = state_ref[g]
        y_off = _dot(cg, st.astype(BF16)) * eac_e[:, g * gw:(g + 1) * gw]
        for p in range(SSD_HPG // 2):
            h0 = g * SSD_HPG + 2 * p
            ms = []
            for hh in (h0, h0 + 1):
                diff = acum[:, hh:hh + 1] - acum_t[hh:hh + 1, :]
                ms.append((cb * jnp.exp(jnp.where(keep, diff, -jnp.inf))).astype(BF16))
            xp = xdt_b[:, h0 * SSD_HEADDIM:(h0 + 2) * SSD_HEADDIM]
            y_d = (_dot(ms[0], jnp.where(lane < SSD_HEADDIM, xp, zero_b))
                   + _dot(ms[1], jnp.where(lane >= SSD_HEADDIM, xp, zero_b)))
            y_ref[0, :, h0 * SSD_HEADDIM:(h0 + 2) * SSD_HEADDIM] = y_d + y_off[:, p * 128:(p + 1) * 128]
        state_ref[g] = st * elast_e[:, g * gw:(g + 1) * gw] + _dot_tn(bg, xdec_b[:, g * gw:(g + 1) * gw])

    @pl.when(ci == nc - 1)
    def _():
        fin_ref[0] = state_ref[...]


def _ssd_scan(xs, bm, cm, dt, dt_t, a_dir, init, expand_mat, reverse):
    b, t, _ = xs.shape
    nc = t // SSD_CHUNK
    nh2 = 2 * SSD_HEADS
    chunk = (lambda i, c: (i, nc - 1 - c, 0)) if reverse else (lambda i, c: (i, c, 0))
    chunk_t = (lambda i, c: (i, 0, nc - 1 - c)) if reverse else (lambda i, c: (i, 0, c))
    full = lambda shape: pl.BlockSpec(shape, lambda i, c: (0,) * len(shape))
    st_shape = (SSD_GROUPS, SSD_STATE, SSD_GROUP_W)
    st_spec = pl.BlockSpec((1,) + st_shape, lambda i, c: (i, 0, 0, 0))
    return pl.pallas_call(
        functools.partial(_ssd_scan_kernel, reverse=reverse),
        grid=(b, nc),
        in_specs=[pl.BlockSpec((1, SSD_CHUNK, SSD_INNER), chunk),
                  pl.BlockSpec((1, SSD_CHUNK, SSD_GN), chunk),
                  pl.BlockSpec((1, SSD_CHUNK, SSD_GN), chunk),
                  pl.BlockSpec((1, SSD_CHUNK, nh2), chunk),
                  pl.BlockSpec((1, nh2, SSD_CHUNK), chunk_t),
                  full((1, SSD_HEADS)), full((SSD_HEADS, 1)), st_spec, full(expand_mat.shape)],
        out_specs=[pl.BlockSpec((1, SSD_CHUNK, SSD_INNER), chunk), st_spec],
        out_shape=[jax.ShapeDtypeStruct((b, t, SSD_INNER), F32),
                   jax.ShapeDtypeStruct((b,) + st_shape, F32)],
        scratch_shapes=[pltpu.VMEM(st_shape, F32)],
        compiler_params=_cparams("parallel", "arbitrary"),
        name="ssd_scan_bwd" if reverse else "ssd_scan_fwd",
    )(xs, bm, cm, dt, dt_t, a_dir.reshape(1, -1), a_dir.reshape(-1, 1), init, expand_mat)


def _ssd_out_kernel(yf_ref, yb_ref, xs_ref, z_ref, dexp_ref, gn_ref, w_ref, x_ref, gt_ref, o_ref):
    y = yf_ref[0] + yb_ref[0] + xs_ref[0].astype(F32) * dexp_ref[...]
    gated = y * _silu(z_ref[0].astype(F32))
    gn = gn_ref[...]
    gw = SSD_GROUP_W
    parts = [(_rms(gated[:, g * gw:(g + 1) * gw]) * gn[:, g * gw:(g + 1) * gw]).astype(BF16)
             for g in range(SSD_GROUPS)]
    normed = jnp.concatenate(parts, axis=-1)
    o_ref[0] = x_ref[0] + gt_ref[0] * _dot(normed, w_ref[...])


def _ssd_out(yf, yb, xs, z, d_exp, g_norm, w_out, x, gate, tm):
    b, t, d = x.shape
    tok = lambda w: pl.BlockSpec((1, tm, w), lambda i, j: (i, j, 0))
    full = lambda shape: pl.BlockSpec(shape, lambda i, j: (0,) * len(shape))
    return pl.pallas_call(
        _ssd_out_kernel,
        grid=(b, t // tm),
        in_specs=[tok(SSD_INNER), tok(SSD_INNER), tok(SSD_INNER), tok(SSD_INNER),
                  full((1, SSD_INNER)), full((1, SSD_INNER)), full(w_out.shape), tok(d),
                  pl.BlockSpec((1, 1, d), lambda i, j: (i, 0, 0))],
        out_specs=tok(d),
        out_shape=jax.ShapeDtypeStruct((b, t, d), F32),
        compiler_params=_cparams("parallel", "parallel"),
        name="ssd_out",
    )(yf, yb, xs, z, d_exp, g_norm.reshape(1, -1), w_out, x, gate)


def _ssd_mixer(x, cx, mods_l, mods_c, g_mix, w_in, conv_w, conv_b, dt_bias, a_log, d_skip, g_norm, w_out, need_ctx):
    b = x.shape[0]
    wz = w_in[:, :SSD_INNER].astype(BF16)
    wxbc = w_in[:, SSD_INNER:SSD_INNER + SSD_CONV_DIM].astype(BF16)
    wdt = w_in[:, SSD_INNER + SSD_CONV_DIM:].astype(BF16)
    wdt_t = wdt.T
    w_out_b = w_out.astype(BF16)
    a = -jnp.exp(a_log.astype(F32))
    d_exp = jnp.repeat(d_skip.astype(F32), SSD_HEADDIM).reshape(1, SSD_INNER)
    expand_mat = jnp.repeat(jnp.eye(SSD_HEADS, dtype=BF16), SSD_HEADDIM, axis=1)
    zero = jnp.zeros((b, SSD_GROUPS, SSD_STATE, SSD_GROUP_W), F32)

    def run(xx, mm, init_f, init_b):
        t = xx.shape[1]
        z, xs, bm, cm, dt, dt_t = _ssd_in(xx, g_mix, mm[0], mm[1], wz, wxbc, wdt, wdt_t, conv_w, conv_b,
                                          dt_bias, min(256, t))
        yf, s_f = _ssd_scan(xs, bm, cm, dt, dt_t, a[0], init_f, expand_mat, False)
        yb, s_b = _ssd_scan(xs, bm, cm, dt, dt_t, a[1], init_b, expand_mat, True)
        return (z, xs, yf, yb), s_f, s_b

    parts_c, s_f, s_b = run(cx, mods_c, zero, zero)
    parts_l, _, _ = run(x, mods_l, s_f, s_b)
    finish = lambda parts, xx, mm: _ssd_out(parts[2], parts[3], parts[1], parts[0], d_exp, g_norm, w_out_b,
                                            xx, mm[2], min(256, xx.shape[1]))
    x_new = finish(parts_l, x, mods_l)
    cx_new = finish(parts_c, cx, mods_c) if need_ctx else None
    return x_new, cx_new


def _fnet_feat_kernel(x_ref, g_ref, sh_ref, sc_ref, cs_ref, z_ref):
    h = _prenorm(x_ref[0], g_ref[...], sh_ref[0], sc_ref[0]).astype(BF16)
    gd = cs_ref.shape[0]
    for g in range(FNET_GROUPS):
        r = _dot(h[:, g * gd:(g + 1) * gd], cs_ref[...])
        z_ref[0, 0, :, g * gd:(g + 1) * gd] = r[:, :gd].astype(BF16)
        z_ref[0, 1, :, g * gd:(g + 1) * gd] = r[:, gd:].astype(BF16)


def _fnet_feat(x, g_mix, shift, scale, cs, tm):
    b, t, d = x.shape
    return pl.pallas_call(
        _fnet_feat_kernel,
        grid=(b, t // tm),
        in_specs=[pl.BlockSpec((1, tm, d), lambda i, j: (i, j, 0)),
                  pl.BlockSpec((1, d), lambda i, j: (0, 0)),
                  pl.BlockSpec((1, 1, d), lambda i, j: (i, 0, 0)),
                  pl.BlockSpec((1, 1, d), lambda i, j: (i, 0, 0)),
                  pl.BlockSpec(cs.shape, lambda i, j: (0, 0))],
        out_specs=pl.BlockSpec((1, 2, tm, d), lambda i, j: (i, 0, j, 0)),
        out_shape=jax.ShapeDtypeStruct((b, 2, t, d), BF16),
        compiler_params=_cparams("parallel", "parallel"),
        name="fnet_feature_dft",
    )(x, g_mix.reshape(1, d), shift, scale, cs)


def _bmm_kernel(a_ref, z_ref, o_ref, acc_ref):
    k = pl.program_id(3)

    @pl.when(k == 0)
    def _():
        acc_ref[...] = jnp.zeros_like(acc_ref)

    acc_ref[...] += _dot(a_ref[...], z_ref[0])

    @pl.when(k == pl.num_programs(3) - 1)
    def _():
        o_ref[0] = acc_ref[...].astype(o_ref.dtype)


def _bmm(a, z, tm, tn, tk):
    m, kk = a.shape
    b, _, n = z.shape
    return pl.pallas_call(
        _bmm_kernel,
        grid=(b, m // tm, n // tn, kk // tk),
        in_specs=[pl.BlockSpec((tm, tk), lambda i, r, c, k: (r, k)),
                  pl.BlockSpec((1, tk, tn), lambda i, r, c, k: (i, k, c))],
        out_specs=pl.BlockSpec((1, tm, tn), lambda i, r, c, k: (i, r, c)),
        out_shape=jax.ShapeDtypeStruct((b, m, n), BF16),
        scratch_shapes=[pltpu.VMEM((tm, tn), F32)],
        compiler_params=_cparams("parallel", "parallel", "parallel", "arbitrary"),
        name="fnet_token_dft",
    )(a, z)


def _dft_cos_sin(n):
    lo = min(n, 128)
    hi = n // lo
    k = jnp.arange(n, dtype=jnp.int32)[:, None]
    w = 2.0 * math.pi / n
    ang_hi = ((k * (jnp.arange(hi, dtype=jnp.int32) * lo)[None, :]) % n).astype(F32) * w
    ang_lo = ((k * jnp.arange(lo, dtype=jnp.int32)[None, :]) % n).astype(F32) * w
    c1, s1 = jnp.cos(ang_hi)[:, :, None], jnp.sin(ang_hi)[:, :, None]
    c2, s2 = jnp.cos(ang_lo)[:, None, :], jnp.sin(ang_lo)[:, None, :]
    return (c1 * c2 - s1 * s2).reshape(n, n), (s1 * c2 + c1 * s2).reshape(n, n)


def _fnet_mixer(x, mods, g_mix, cs_feat, w_o_b):
    b, t, d = x.shape
    tm = min(512, t)
    z = _fnet_feat(x, g_mix, mods[0], mods[1], cs_feat, tm)
    ct, st = _dft_cos_sin(t)
    f = (jnp.concatenate([ct, -st], axis=1) * (t ** -0.5)).astype(BF16)
    mixed = _bmm(f, z.reshape(b, 2 * t, d), min(1024, t), d, 512)
    return _proj_residual(mixed, w_o_b, x, mods[2], tm)


def _router_kernel(x_ref, g_ref, sh_ref, sc_ref, wrT_ref, h_ref, aff_ref):
    h = _prenorm(x_ref[0], g_ref[...], sh_ref[0], sc_ref[0]).astype(BF16)
    h_ref[0] = h
    logits = _dot_nt(wrT_ref[...], h)
    m = jnp.max(logits, axis=0, keepdims=True)
    e = jnp.exp(logits - m)
    aff_ref[0] = e / jnp.sum(e, axis=0, keepdims=True)


def _router(x, g_ffn, shift, scale, w_router_t, tm):
    b, t, d = x.shape
    ne = w_router_t.shape[0]
    return pl.pallas_call(
        _router_kernel,
        grid=(b, t // tm),
        in_specs=[pl.BlockSpec((1, tm, d), lambda i, j: (i, j, 0)),
                  pl.BlockSpec((1, d), lambda i, j: (0, 0)),
                  pl.BlockSpec((1, 1, d), lambda i, j: (i, 0, 0)),
                  pl.BlockSpec((1, 1, d), lambda i, j: (i, 0, 0)),
                  pl.BlockSpec((ne, d), lambda i, j: (0, 0))],
        out_specs=[pl.BlockSpec((1, tm, d), lambda i, j: (i, j, 0)),
                   pl.BlockSpec((1, ne, tm), lambda i, j: (i, 0, j))],
        out_shape=[jax.ShapeDtypeStruct((b, t, d), BF16),
                   jax.ShapeDtypeStruct((b, ne, t), F32)],
        compiler_params=_cparams("parallel", "parallel"),
        name="moe_router",
    )(x, g_ffn.reshape(1, d), shift, scale, w_router_t)


def _expert_kernel(xe_ref, gate_ref, wg_ref, wu_ref, wd_ref, o_ref):
    xe = xe_ref[0]
    hid = _silu(_dot(xe, wg_ref[0])) * _dot(xe, wu_ref[0])
    o_ref[0] = _dot(hid.astype(BF16), wd_ref[0]) * gate_ref[0]


def _experts(xe, gate, wg, wu, wd, tr):
    ne, r, d = xe.shape
    f = wg.shape[-1]
    return pl.pallas_call(
        _expert_kernel,
        grid=(ne, r // tr),
        in_specs=[pl.BlockSpec((1, tr, d), lambda e, j: (e, j, 0)),
                  pl.BlockSpec((1, tr, 1), lambda e, j: (e, j, 0)),
                  pl.BlockSpec((1, d, f), lambda e, j: (e, 0, 0)),
                  pl.BlockSpec((1, d, f), lambda e, j: (e, 0, 0)),
                  pl.BlockSpec((1, f, d), lambda e, j: (e, 0, 0))],
        out_specs=pl.BlockSpec((1, tr, d), lambda e, j: (e, j, 0)),
        out_shape=jax.ShapeDtypeStruct((ne, r, d), F32),
        compiler_params=_cparams("parallel", "parallel"),
        name="moe_experts",
    )(xe, gate, wg, wu, wd)


def _ec_moe(x, mods, g_ffn, w_router_t, wg, wu, wd):
    b, t, d = x.shape
    ne = w_router_t.shape[0]
    cap = CAPACITY_FACTOR * t // ne
    h, aff_t = _router(x, g_ffn, mods[3], mods[4], w_router_t, min(512, t))
    gate, idx = lax.top_k(aff_t, cap)
    xe = jax.vmap(lambda hb, ib: hb[ib])(h, idx)
    xe = xe.transpose(1, 0, 2, 3).reshape(ne, b * cap, d)
    gate_e = gate.transpose(1, 0, 2).reshape(ne, b * cap, 1)
    ye = _experts(xe, gate_e, wg, wu, wd, min(1024, b * cap))
    ye = ye.reshape(ne, b, cap, d).transpose(1, 0, 2, 3)
    out = jax.vmap(lambda ib, yb: jnp.zeros((t, d), F32).at[ib.reshape(-1)].add(yb.reshape(-1, d)))(idx, ye)
    return x + mods[5] * out


def kernel(x, c, ctx, c_ctx, w_mod, b_mod, g_mix, g_ffn, mla_w_in, mla_g_q, mla_w_uq, mla_g_kv, mla_w_ukv,
           mla_g_qn, mla_g_kn, mla_w_o, ssd_w_in, ssd_conv_w, ssd_conv_b, ssd_dt_bias, ssd_a_log, ssd_d,
           ssd_g_norm, ssd_w_out, fnet_w_o, moe_w_router, moe_w_gate, moe_w_up, moe_w_down):
    depth = w_mod.shape[0]
    b, t, d = x.shape
    cx = ctx
    rows = 16
    cc = jnp.concatenate([c, c_ctx[None, :], jnp.zeros((rows - b - 1, d), F32)], axis=0)
    mods = _mods(cc, w_mod, b_mod)
    for i in range(depth):
        kind, j = i % 3, i // 3
        last = i == depth - 1
        ml = mods[i, :b].reshape(b, 1, N_MOD, d)
        mods_l = [ml[:, :, n] for n in range(N_MOD)]
        mc = jnp.broadcast_to(mods[i, b].reshape(1, 1, N_MOD, d), (b, 1, N_MOD, d))
        mods_c = [mc[:, :, n] for n in range(N_MOD)]
        if kind == 0:
            x, cx_new = _mla_mixer(x, cx, i, mods_l, mods_c, g_mix[i], mla_w_in[j], mla_g_q[j], mla_w_uq[j],
                                   mla_g_kv[j], mla_w_ukv[j], mla_g_qn[j], mla_g_kn[j], mla_w_o[j], not last)
        elif kind == 1:
            x, cx_new = _ssd_mixer(x, cx, mods_l, mods_c, g_mix[i], ssd_w_in[j], ssd_conv_w[j], ssd_conv_b[j],
                                   ssd_dt_bias[j], ssd_a_log[j], ssd_d[j], ssd_g_norm[j], ssd_w_out[j], not last)
        else:
            gd = d // FNET_GROUPS
            cd, sd = _dft_cos_sin(gd)
            cs_feat = (jnp.concatenate([cd, sd], axis=1) * (gd ** -0.5)).astype(BF16)
            w_o_b = fnet_w_o[j].astype(BF16)
            x_new = _fnet_mixer(x, mods_l, g_mix[i], cs_feat, w_o_b)
            cx_new = _fnet_mixer(cx, mods_c, g_mix[i], cs_feat, w_o_b) if not last else None
            x = x_new
        w_r_t = moe_w_router[i].T.astype(BF16)
        wg, wu, wd = moe_w_gate[i].astype(BF16), moe_w_up[i].astype(BF16), moe_w_down[i].astype(BF16)
        x = _ec_moe(x, mods_l, g_ffn[i], w_r_t, wg, wu, wd)
        if not last:
            cx = _ec_moe(cx_new, mods_c, g_ffn[i], w_r_t, wg, wu, wd)
    return x
```

```python
import functools
import math

import jax
import jax.numpy as jnp
from jax import lax
from jax.experimental import pallas as pl
from jax.experimental.pallas import tpu as pltpu

F32 = jnp.float32
BF16 = jnp.bfloat16

NORM_EPS = 1e-6
GRID_W = 64
N_MOD = 6
ROPE_THETA = 10000.0

MLA_HEADS = 8
MLA_Q_LORA = 256
MLA_KV_LORA = 128
MLA_NOPE = 128
MLA_ROPE = 64
MLA_QK_DIM = MLA_NOPE + MLA_ROPE
MLA_V = 128
MLA_DK_PAD = 256

SSD_HEADDIM = 64
SSD_HEADS = 32
SSD_GROUPS = 4
SSD_HPG = SSD_HEADS // SSD_GROUPS
SSD_STATE = 128
SSD_CHUNK = 128
SSD_INNER = SSD_HEADS * SSD_HEADDIM
SSD_GN = SSD_GROUPS * SSD_STATE
SSD_CONV_DIM = SSD_INNER + 2 * SSD_GN
SSD_GROUP_W = SSD_INNER // SSD_GROUPS

FNET_GROUPS = 4
N_EXPERTS = 16
CAPACITY_FACTOR = 2

VMEM_LIMIT_BYTES = 52 * 1024 * 1024
HALO = 8


def _cparams(*sem):
    return pltpu.CompilerParams(dimension_semantics=sem, vmem_limit_bytes=VMEM_LIMIT_BYTES)


def _dot(a, b):
    return jnp.dot(a, b, preferred_element_type=F32)


def _dot_nt(a, b):
    return lax.dot_general(a, b, (((1,), (1,)), ((), ())), preferred_element_type=F32)


def _dot_tn(a, b):
    return lax.dot_general(a, b, (((0,), (0,)), ((), ())), preferred_element_type=F32)


def _split2(x):
    hi = x.astype(BF16)
    lo = (x - hi.astype(F32)).astype(BF16)
    return hi, lo


def _split3(x):
    x1 = x.astype(BF16)
    r = x - x1.astype(F32)
    x2 = r.astype(BF16)
    x3 = (r - x2.astype(F32)).astype(BF16)
    return x1, x2, x3


def _rms(x):
    return x * lax.rsqrt(jnp.mean(x * x, axis=-1, keepdims=True) + NORM_EPS)


def _prenorm(x, g, shift, scale):
    return (_rms(x) * g) * (1.0 + scale) + shift


def _silu(x):
    return x * jax.nn.sigmoid(x)


def _softplus(x):
    return jnp.maximum(x, 0.0) + jnp.log1p(jnp.exp(-jnp.abs(x)))


def _mod_kernel(cc_ref, w_ref, b_ref, o_ref):
    s = _silu(cc_ref[...])
    s_hi, s_lo = _split2(s)
    w_hi, w_lo = _split2(w_ref[0])
    acc = _dot(s_hi, w_hi) + _dot(s_lo, w_hi) + _dot(s_hi, w_lo)
    o_ref[0] = acc + b_ref[0]


def _mods(cc, w_mod, b_mod):
    depth, d, n = w_mod.shape
    rows = cc.shape[0]
    tn = 1536
    return pl.pallas_call(
        _mod_kernel,
        grid=(depth, n // tn),
        in_specs=[pl.BlockSpec((rows, d), lambda l, j: (0, 0)),
                  pl.BlockSpec((1, d, tn), lambda l, j: (l, 0, j)),
                  pl.BlockSpec((1, 1, tn), lambda l, j: (l, 0, j))],
        out_specs=pl.BlockSpec((1, rows, tn), lambda l, j: (l, 0, j)),
        out_shape=jax.ShapeDtypeStruct((depth, rows, n), F32),
        compiler_params=_cparams("parallel", "parallel"),
        name="mods",
    )(cc, w_mod, b_mod.reshape(depth, 1, n))


def _proj_res_kernel(a_ref, w_ref, x_ref, gt_ref, o_ref):
    o_ref[0] = x_ref[0] + gt_ref[0] * _dot(a_ref[0], w_ref[...])


def _proj_residual(a, w, x, gate, tm):
    b, t, d = x.shape
    din = a.shape[-1]
    return pl.pallas_call(
        _proj_res_kernel,
        grid=(b, t // tm),
        in_specs=[pl.BlockSpec((1, tm, din), lambda i, j: (i, j, 0)),
                  pl.BlockSpec((din, d), lambda i, j: (0, 0)),
                  pl.BlockSpec((1, tm, d), lambda i, j: (i, j, 0)),
                  pl.BlockSpec((1, 1, d), lambda i, j: (i, 0, 0))],
        out_specs=pl.BlockSpec((1, tm, d), lambda i, j: (i, j, 0)),
        out_shape=jax.ShapeDtypeStruct((b, t, d), F32),
        compiler_params=_cparams("parallel", "parallel"),
        name="proj_residual",
    )(a, w, x, gate)


def _mla_proj_kernel(x_ref, g_ref, sh_ref, sc_ref, win_ref, gq_ref, wuq_ref, gkv_ref, wukv_ref,
                     gqn_ref, gkn_ref, cos_ref, sin_ref, q_ref, k_ref, v_ref):
    h = _prenorm(x_ref[0], g_ref[...], sh_ref[0], sc_ref[0])
    a = _dot(h.astype(BF16), win_ref[...])
    qa = a[:, :MLA_Q_LORA]
    kva = a[:, MLA_Q_LORA:MLA_Q_LORA + MLA_KV_LORA]
    kr = a[:, 384:512]
    krs = a[:, 512:640]
    q = _dot((_rms(qa) * gq_ref[...]).astype(BF16), wuq_ref[...])
    kv = _dot((_rms(kva) * gkv_ref[...]).astype(BF16), wukv_ref[...])
    cos_t = cos_ref[...]
    sin_t = sin_ref[...]
    gqn = gqn_ref[...]
    gkn = gkn_ref[...]
    scale = MLA_QK_DIM ** -0.5 * math.log2(math.e)
    inv_dim = 1.0 / MLA_QK_DIM
    nh = MLA_HEADS
    kr_rot = kr * gkn[1:2] * cos_t + krs * gkn[2:3] * sin_t
    kr_sq = kr * kr
    for hd in range(nh):
        qn = q[:, hd * 128:(hd + 1) * 128]
        qr = q[:, (nh + hd) * 128:(nh + hd + 1) * 128]
        qrs = q[:, (2 * nh + hd) * 128:(2 * nh + hd + 1) * 128]
        rq = lax.rsqrt(jnp.sum(qn * qn + qr * qr, axis=-1, keepdims=True) * inv_dim + NORM_EPS) * scale
        q_ref[0, hd, :, 0:128] = (qn * gqn[0:1] * rq).astype(BF16)
        q_ref[0, hd, :, 128:256] = ((qr * gqn[1:2] * cos_t + qrs * gqn[2:3] * sin_t) * rq).astype(BF16)
        kn = kv[:, hd * 256:hd * 256 + 128]
        rk = lax.rsqrt(jnp.sum(kn * kn + kr_sq, axis=-1, keepdims=True) * inv_dim + NORM_EPS)
        k_ref[0, hd, :, 0:128] = (kn * gkn[0:1] * rk).astype(BF16)
        k_ref[0, hd, :, 128:256] = (kr_rot * rk).astype(BF16)
        v_ref[0, hd] = kv[:, hd * 256 + 128:(hd + 1) * 256].astype(BF16)


def _swap_halves(w):
    half = w.shape[-1] // 2
    return jnp.concatenate([w[..., half:], w[..., :half]], axis=-1)


def _pad_last(w, n):
    return jnp.concatenate([w, jnp.zeros(w.shape[:-1] + (n - w.shape[-1],), w.dtype)], axis=-1)


def _mla_weights(w_in, w_uq, g_qn, g_kn):
    d = w_in.shape[0]
    kr = w_in[:, MLA_Q_LORA + MLA_KV_LORA:]
    w_in_ext = jnp.concatenate(
        [w_in[:, :MLA_Q_LORA + MLA_KV_LORA], _pad_last(kr, 128), _pad_last(_swap_halves(kr), 128)], axis=1)
    wq = w_uq.reshape(MLA_Q_LORA, MLA_HEADS, MLA_QK_DIM)
    nope = wq[:, :, :MLA_NOPE].reshape(MLA_Q_LORA, -1)
    rope = wq[:, :, MLA_NOPE:]
    rope_p = _pad_last(rope, 128).reshape(MLA_Q_LORA, -1)
    rope_s = _pad_last(_swap_halves(rope), 128).reshape(MLA_Q_LORA, -1)
    w_uq_ext = jnp.concatenate([nope, rope_p, rope_s], axis=1)

    def gains(g):
        return jnp.stack([g[:MLA_NOPE], _pad_last(g[MLA_NOPE:], 128), _pad_last(_swap_halves(g[MLA_NOPE:]), 128)])

    return w_in_ext.astype(BF16), w_uq_ext.astype(BF16), gains(g_qn), gains(g_kn)


def _rope_tables(t, with_position):
    if not with_position:
        cos_t = _pad_last(jnp.ones((t, MLA_ROPE), F32), 128)
        return cos_t, jnp.zeros((t, 128), F32)
    rows_n = t // GRID_W
    nfreq = MLA_ROPE // 4
    row = jnp.repeat(jnp.arange(rows_n, dtype=F32), GRID_W)
    col = jnp.tile(jnp.arange(GRID_W, dtype=F32), rows_n)
    inv = ROPE_THETA ** (-jnp.arange(nfreq, dtype=F32) / nfreq)
    ang = jnp.concatenate([row[:, None] * inv, col[:, None] * inv], axis=-1)
    cos, sin = jnp.cos(ang), jnp.sin(ang)
    return (_pad_last(jnp.concatenate([cos, cos], axis=-1), 128),
            _pad_last(jnp.concatenate([-sin, sin], axis=-1), 128))


def _mla_project(x, g_mix, shift, scale, w_in_ext, g_q, w_uq_ext, g_kv, w_ukv, gqn3, gkn3, cos_t, sin_t, tm):
    b, t, d = x.shape
    nh = MLA_HEADS
    full = lambda shape: pl.BlockSpec(shape, lambda i, j: (0,) * len(shape))
    per_b = pl.BlockSpec((1, 1, d), lambda i, j: (i, 0, 0))
    return pl.pallas_call(
        _mla_proj_kernel,
        grid=(b, t // tm),
        in_specs=[pl.BlockSpec((1, tm, d), lambda i, j: (i, j, 0)),
                  full((1, d)), per_b, per_b,
                  full(w_in_ext.shape), full((1, MLA_Q_LORA)), full(w_uq_ext.shape),
                  full((1, MLA_KV_LORA)), full(w_ukv.shape), full((3, 128)), full((3, 128)),
                  pl.BlockSpec((tm, 128), lambda i, j: (j, 0)),
                  pl.BlockSpec((tm, 128), lambda i, j: (j, 0))],
        out_specs=[pl.BlockSpec((1, nh, tm, MLA_DK_PAD), lambda i, j: (i, 0, j, 0)),
                   pl.BlockSpec((1, nh, tm, MLA_DK_PAD), lambda i, j: (i, 0, j, 0)),
                   pl.BlockSpec((1, nh, tm, MLA_V), lambda i, j: (i, 0, j, 0))],
        out_shape=[jax.ShapeDtypeStruct((b, nh, t, MLA_DK_PAD), BF16),
                   jax.ShapeDtypeStruct((b, nh, t, MLA_DK_PAD), BF16),
                   jax.ShapeDtypeStruct((b, nh, t, MLA_V), BF16)],
        compiler_params=_cparams("parallel", "parallel"),
        name="mla_project",
    )(x, g_mix.reshape(1, d), shift, scale, w_in_ext, g_q.reshape(1, -1), w_uq_ext,
      g_kv.reshape(1, -1), w_ukv, gqn3, gkn3, cos_t, sin_t)


def _flash_kernel(*refs, n_src, tk):
    q_ref = refs[0]
    o_ref = refs[1 + 2 * n_src]
    q = q_ref[0, 0]
    tq = q.shape[0]

    def scores(k_ref, off, tks):
        return _dot_nt(q, k_ref[0, 0, pl.ds(off, tks), :])

    def update(s, v, m, l, acc):
        m_new = jnp.maximum(m, jnp.max(s, axis=-1, keepdims=True))
        alpha = jnp.exp2(m - m_new)
        p = jnp.exp2(s - m_new)
        l = alpha * l + jnp.sum(p, axis=-1, keepdims=True)
        acc = alpha * acc + _dot(p.astype(BF16), v)
        return m_new, l, acc

    srcs = []
    for s_i in range(n_src):
        k_ref, v_ref = refs[1 + 2 * s_i], refs[2 + 2 * s_i]
        tks = min(tk, k_ref.shape[2])
        srcs.append((k_ref, v_ref, tks, k_ref.shape[2] // tks))

    m = jnp.full((tq, 1), -jnp.inf, F32)
    l = jnp.zeros((tq, 1), F32)
    acc = jnp.zeros((tq, MLA_V), F32)
    s = scores(srcs[0][0], 0, srcs[0][2])
    for s_i, (k_ref, v_ref, tks, n) in enumerate(srcs):
        if n > 1:
            def body(j, c, k_ref=k_ref, v_ref=v_ref, tks=tks):
                m, l, acc, s = c
                s_next = scores(k_ref, pl.multiple_of((j + 1) * tks, tks), tks)
                v = v_ref[0, 0, pl.ds(pl.multiple_of(j * tks, tks), tks), :]
                return update(s, v, m, l, acc) + (s_next,)
            m, l, acc, s = lax.fori_loop(0, n - 1, body, (m, l, acc, s))
        s_next = scores(srcs[s_i + 1][0], 0, srcs[s_i + 1][2]) if s_i + 1 < n_src else None
        m, l, acc = update(s, v_ref[0, 0, pl.ds((n - 1) * tks, tks), :], m, l, acc)
        s = s_next
    o_ref[0] = (acc / l).astype(BF16)


def _flash(q, kvs, tq, tk):
    b, nh, t, dk = q.shape
    in_specs = [pl.BlockSpec((1, 1, tq, dk), lambda i, h, j: (i, h, j, 0))]
    args = [q]
    for k, v in kvs:
        ts = k.shape[2]
        in_specs.append(pl.BlockSpec((1, 1, ts, dk), lambda i, h, j: (i, h, 0, 0)))
        in_specs.append(pl.BlockSpec((1, 1, ts, MLA_V), lambda i, h, j: (i, h, 0, 0)))
        args += [k, v]
    return pl.pallas_call(
        functools.partial(_flash_kernel, n_src=len(kvs), tk=tk),
        grid=(b, nh, t // tq),
        in_specs=in_specs,
        out_specs=pl.BlockSpec((1, tq, MLA_V), lambda i, h, j: (i, j, h)),
        out_shape=jax.ShapeDtypeStruct((b, t, nh * MLA_V), BF16),
        compiler_params=_cparams("parallel", "parallel", "parallel"),
        name="mla_attention",
    )(*args)


def _mla_mixer(x, cx, i, mods_l, mods_c, g_mix, w_in, g_q, w_uq, g_kv, w_ukv, g_qn, g_kn, w_o, need_ctx):
    b, t, d = x.shape
    tc = cx.shape[1]
    w_in_ext, w_uq_ext, gqn3, gkn3 = _mla_weights(w_in, w_uq, g_qn, g_kn)
    w_ukv_b = w_ukv.astype(BF16)
    w_o_b = w_o.astype(BF16)
    cos_l, sin_l = _rope_tables(t, True)
    cos_c, sin_c = _rope_tables(tc, False)
    proj = lambda xx, mm, cs, sn, tm: _mla_project(xx, g_mix, mm[0], mm[1], w_in_ext, g_q, w_uq_ext, g_kv,
                                                   w_ukv_b, gqn3, gkn3, cs, sn, tm)
    ql, kl, vl = proj(x, mods_l, cos_l, sin_l, min(256, t))
    qc, kc, vc = proj(cx, mods_c, cos_c, sin_c, min(256, tc))
    ol = _flash(ql, [(kc, vc), (kl, vl)], min(1024, t), 512)
    x_new = _proj_residual(ol, w_o_b, x, mods_l[2], min(512, t))
    cx_new = None
    if need_ctx:
        oc = _flash(qc, [(kc, vc)], min(256, tc), 512)
        cx_new = _proj_residual(oc, w_o_b, cx, mods_c[2], min(256, tc))
    return x_new, cx_new


def _ssd_in_kernel(xm_ref, xp_ref, xn_ref, g_ref, sh_ref, sc_ref, wz_ref, wxbc_ref, wdt_ref, wdtT_ref,
                   cw_ref, cb_ref, dtb_ref, dtbT_ref,
                   z_ref, xs_ref, b_ref, c_ref, dt_ref, dtT_ref, hall_ref):
    i = pl.program_id(1)
    n = pl.num_programs(1)
    tm = xm_ref.shape[1]
    g, sh, sc = g_ref[...], sh_ref[0], sc_ref[0]
    hm = _prenorm(xm_ref[0], g, sh, sc)
    hp = jnp.where(i > 0, _prenorm(xp_ref[0], g, sh, sc), 0.0)
    hn = jnp.where(i < n - 1, _prenorm(xn_ref[0], g, sh, sc), 0.0)
    hall_ref[0:HALO] = hp
    hall_ref[HALO:HALO + tm] = hm
    hall_ref[HALO + tm:2 * HALO + tm] = hn
    raw = _dot(hall_ref[...].astype(BF16), wxbc_ref[...])
    rows = tm + 2 * HALO
    up = pltpu.roll(raw, 1, 0)
    dn = pltpu.roll(raw, rows - 1, 0)
    cw = cw_ref[...]
    conv = (up * cw[0:1] + raw * cw[1:2] + dn * cw[2:3])[HALO:HALO + tm] + cb_ref[...]
    act = _silu(conv)
    xs_ref[0] = act[:, :SSD_INNER].astype(BF16)
    b_ref[0] = act[:, SSD_INNER:SSD_INNER + SSD_GN].astype(BF16)
    c_ref[0] = act[:, SSD_INNER + SSD_GN:].astype(BF16)
    hmb = hm.astype(BF16)
    z_ref[0] = _dot(hmb, wz_ref[...]).astype(BF16)
    dt_ref[0] = _softplus(_dot(hmb, wdt_ref[...]) + dtb_ref[...])
    dtT_ref[0] = _softplus(_dot_nt(wdtT_ref[...], hmb) + dtbT_ref[...])


def _ssd_in(x, g_mix, shift, scale, wz, wxbc, wdt, wdtT, conv_w, conv_b, dt_bias, tm):
    b, t, d = x.shape
    nb8 = tm // HALO
    nh2 = 2 * SSD_HEADS
    full = lambda shape: pl.BlockSpec(shape, lambda i, j: (0,) * len(shape))
    per_b = pl.BlockSpec((1, 1, d), lambda i, j: (i, 0, 0))
    tok = lambda w: pl.BlockSpec((1, tm, w), lambda i, j: (i, j, 0))
    return pl.pallas_call(
        _ssd_in_kernel,
        grid=(b, t // tm),
        in_specs=[pl.BlockSpec((1, tm, d), lambda i, j: (i, j, 0)),
                  pl.BlockSpec((1, HALO, d), lambda i, j: (i, jnp.maximum(j * nb8 - 1, 0), 0)),
                  pl.BlockSpec((1, HALO, d), lambda i, j: (i, jnp.minimum((j + 1) * nb8, t // HALO - 1), 0)),
                  full((1, d)), per_b, per_b,
                  full(wz.shape), full(wxbc.shape), full(wdt.shape), full(wdtT.shape),
                  full((3, SSD_CONV_DIM)), full((1, SSD_CONV_DIM)), full((1, nh2)), full((nh2, 1))],
        out_specs=[tok(SSD_INNER), tok(SSD_INNER), tok(SSD_GN), tok(SSD_GN), tok(nh2),
                   pl.BlockSpec((1, nh2, tm), lambda i, j: (i, 0, j))],
        out_shape=[jax.ShapeDtypeStruct((b, t, SSD_INNER), BF16),
                   jax.ShapeDtypeStruct((b, t, SSD_INNER), BF16),
                   jax.ShapeDtypeStruct((b, t, SSD_GN), BF16),
                   jax.ShapeDtypeStruct((b, t, SSD_GN), BF16),
                   jax.ShapeDtypeStruct((b, t, nh2), F32),
                   jax.ShapeDtypeStruct((b, nh2, t), F32)],
        scratch_shapes=[pltpu.VMEM((tm + 2 * HALO, d), F32)],
        compiler_params=_cparams("parallel", "parallel"),
        name="ssd_in",
    )(x, x, x, g_mix.reshape(1, d), shift, scale, wz, wxbc, wdt, wdtT, conv_w, conv_b.reshape(1, -1),
      dt_bias.reshape(1, nh2), dt_bias.reshape(nh2, 1))


def _ssd_scan_kernel(xs_ref, b_ref, c_ref, dt_ref, dtT_ref, arow_ref, acol_ref, init_ref, e_ref,
                     y_ref, fin_ref, state_ref, *, reverse):
    ci = pl.program_id(1)
    nc = pl.num_programs(1)
    q = SSD_CHUNK
    nh = SSD_HEADS

    @pl.when(ci == 0)
    def _():
        state_ref[...] = init_ref[0]

    d0 = nh if reverse else 0
    dt = dt_ref[0][:, d0:d0 + nh]
    dt_t = dtT_ref[0][d0:d0 + nh, :]
    a = dt * arow_ref[...]
    a_t = dt_t * acol_ref[...]
    ii = lax.broadcasted_iota(jnp.int32, (q, q), 0)
    jj = lax.broadcasted_iota(jnp.int32, (q, q), 1)
    keep = (jj >= ii) if reverse else (jj <= ii)
    tri = jnp.where(keep, 1.0, 0.0).astype(BF16)
    tri_t = jnp.where((ii >= jj) if reverse else (ii <= jj), 1.0, 0.0).astype(BF16)
    a1, a2, a3 = _split3(a)
    acum = _dot(tri, a1) + _dot(tri, a2) + _dot(tri, a3)
    t1, t2, t3 = _split3(a_t)
    acum_t = _dot(t1, tri_t) + _dot(t2, tri_t) + _dot(t3, tri_t)
    edge = 0 if reverse else q - 1
    e = e_ref[...]

    def expand(v):
        hi, lo = _split2(v)
        return _dot(hi, e) + _dot(lo, e)

    xdt = xs_ref[0].astype(F32) * expand(dt)
    eac_e = expand(jnp.exp(acum))
    dec_e = expand(jnp.exp(acum[edge:edge + 1, :] - acum))
    elast_e = eac_e[edge:edge + 1, :]
    xdt_b = xdt.astype(BF16)
    xdec_b = (xdt * dec_e).astype(BF16)
    lane = lax.broadcasted_iota(jnp.int32, (q, 2 * SSD_HEADDIM), 1)
    zero_b = jnp.zeros((q, 2 * SSD_HEADDIM), BF16)
    gw = SSD_GROUP_W
    for g in range(SSD_GROUPS):
        bg = b_ref[0][:, g * SSD_STATE:(g + 1) * SSD_STATE]
        cg = c_ref[0][:, g * SSD_STATE:(g + 1) * SSD_STATE]
        cb = _dot_nt(cg, bg)
        st = state_ref[g]
        y_off = _dot(cg, st.astype(BF16)) * eac_e[:, g * gw:(g + 1) * gw]
        for p in range(SSD_HPG // 2):
            h0 = g * SSD_HPG + 2 * p
            ms = []
            for hh in (h0, h0 + 1):
                diff = acum[:, hh:hh + 1] - acum_t[hh:hh + 1, :]
                ms.append((cb * jnp.exp(jnp.where(keep, diff, -jnp.inf))).astype(BF16))
            xp = xdt_b[:, h0 * SSD_HEADDIM:(h0 + 2) * SSD_HEADDIM]
            y_d = (_dot(ms[0], jnp.where(lane < SSD_HEADDIM, xp, zero_b))
                   + _dot(ms[1], jnp.where(lane >= SSD_HEADDIM, xp, zero_b)))
            y_ref[0, :, h0 * SSD_HEADDIM:(h0 + 2) * SSD_HEADDIM] = y_d + y_off[:, p * 128:(p + 1) * 128]
        state_ref[g] = st * elast_e[:, g * gw:(g + 1) * gw] + _dot_tn(bg, xdec_b[:, g * gw:(g + 1) * gw])

    @pl.when(ci == nc - 1)
    def _():
        fin_ref[0] = state_ref[...]


def _ssd_scan(xs, bm, cm, dt, dt_t, a_dir, init, expand_mat, reverse):
    b, t, _ = xs.shape
    nc = t // SSD_CHUNK
    nh2 = 2 * SSD_HEADS
    chunk = (lambda i, c: (i, nc - 1 - c, 0)) if reverse else (lambda i, c: (i, c, 0))
    chunk_t = (lambda i, c: (i, 0, nc - 1 - c)) if reverse else (lambda i, c: (i, 0, c))
    full = lambda shape: pl.BlockSpec(shape, lambda i, c: (0,) * len(shape))
    st_shape = (SSD_GROUPS, SSD_STATE, SSD_GROUP_W)
    st_spec = pl.BlockSpec((1,) + st_shape, lambda i, c: (i, 0, 0, 0))
    return pl.pallas_call(
        functools.partial(_ssd_scan_kernel, reverse=reverse),
        grid=(b, nc),
        in_specs=[pl.BlockSpec((1, SSD_CHUNK, SSD_INNER), chunk),
                  pl.BlockSpec((1, SSD_CHUNK, SSD_GN), chunk),
                  pl.BlockSpec((1, SSD_CHUNK, SSD_GN), chunk),
                  pl.BlockSpec((1, SSD_CHUNK, nh2), chunk),
                  pl.BlockSpec((1, nh2, SSD_CHUNK), chunk_t),
                  full((1, SSD_HEADS)), full((SSD_HEADS, 1)), st_spec, full(expand_mat.shape)],
        out_specs=[pl.BlockSpec((1, SSD_CHUNK, SSD_INNER), chunk), st_spec],
        out_shape=[jax.ShapeDtypeStruct((b, t, SSD_INNER), F32),
                   jax.ShapeDtypeStruct((b,) + st_shape, F32)],
        scratch_shapes=[pltpu.VMEM(st_shape, F32)],
        compiler_params=_cparams("parallel", "arbitrary"),
        name="ssd_scan_bwd" if reverse else "ssd_scan_fwd",
    )(xs, bm, cm, dt, dt_t, a_dir.reshape(1, -1), a_dir.reshape(-1, 1), init, expand_mat)


def _ssd_out_kernel(yf_ref, yb_ref, xs_ref, z_ref, dexp_ref, gn_ref, w_ref, x_ref, gt_ref, o_ref):
    y = yf_ref[0] + yb_ref[0] + xs_ref[0].astype(F32) * dexp_ref[...]
    gated = y * _silu(z_ref[0].astype(F32))
    gn = gn_ref[...]
    gw = SSD_GROUP_W
    parts = [(_rms(gated[:, g * gw:(g + 1) * gw]) * gn[:, g * gw:(g + 1) * gw]).astype(BF16)
             for g in range(SSD_GROUPS)]
    normed = jnp.concatenate(parts, axis=-1)
    o_ref[0] = x_ref[0] + gt_ref[0] * _dot(normed, w_ref[...])


def _ssd_out(yf, yb, xs, z, d_exp, g_norm, w_out, x, gate, tm):
    b, t, d = x.shape
    tok = lambda w: pl.BlockSpec((1, tm, w), lambda i, j: (i, j, 0))
    full = lambda shape: pl.BlockSpec(shape, lambda i, j: (0,) * len(shape))
    return pl.pallas_call(
        _ssd_out_kernel,
        grid=(b, t // tm),
        in_specs=[tok(SSD_INNER), tok(SSD_INNER), tok(SSD_INNER), tok(SSD_INNER),
                  full((1, SSD_INNER)), full((1, SSD_INNER)), full(w_out.shape), tok(d),
                  pl.BlockSpec((1, 1, d), lambda i, j: (i, 0, 0))],
        out_specs=tok(d),
        out_shape=jax.ShapeDtypeStruct((b, t, d), F32),
        compiler_params=_cparams("parallel", "parallel"),
        name="ssd_out",
    )(yf, yb, xs, z, d_exp, g_norm.reshape(1, -1), w_out, x, gate)


def _ssd_mixer(x, cx, mods_l, mods_c, g_mix, w_in, conv_w, conv_b, dt_bias, a_log, d_skip, g_norm, w_out, need_ctx):
    b = x.shape[0]
    wz = w_in[:, :SSD_INNER].astype(BF16)
    wxbc = w_in[:, SSD_INNER:SSD_INNER + SSD_CONV_DIM].astype(BF16)
    wdt = w_in[:, SSD_INNER + SSD_CONV_DIM:].astype(BF16)
    wdt_t = wdt.T
    w_out_b = w_out.astype(BF16)
    a = -jnp.exp(a_log.astype(F32))
    d_exp = jnp.repeat(d_skip.astype(F32), SSD_HEADDIM).reshape(1, SSD_INNER)
    expand_mat = jnp.repeat(jnp.eye(SSD_HEADS, dtype=BF16), SSD_HEADDIM, axis=1)
    zero = jnp.zeros((b, SSD_GROUPS, SSD_STATE, SSD_GROUP_W), F32)

    def run(xx, mm, init_f, init_b):
        t = xx.shape[1]
        z, xs, bm, cm, dt, dt_t = _ssd_in(xx, g_mix, mm[0], mm[1], wz, wxbc, wdt, wdt_t, conv_w, conv_b,
                                          dt_bias, min(256, t))
        yf, s_f = _ssd_scan(xs, bm, cm, dt, dt_t, a[0], init_f, expand_mat, False)
        yb, s_b = _ssd_scan(xs, bm, cm, dt, dt_t, a[1], init_b, expand_mat, True)
        return (z, xs, yf, yb), s_f, s_b

    parts_c, s_f, s_b = run(cx, mods_c, zero, zero)
    parts_l, _, _ = run(x, mods_l, s_f, s_b)
    finish = lambda parts, xx, mm: _ssd_out(parts[2], parts[3], parts[1], parts[0], d_exp, g_norm, w_out_b,
                                            xx, mm[2], min(256, xx.shape[1]))
    x_new = finish(parts_l, x, mods_l)
    cx_new = finish(parts_c, cx, mods_c) if need_ctx else None
    return x_new, cx_new


def _fnet_feat_kernel(x_ref, g_ref, sh_ref, sc_ref, cs_ref, z_ref):
    h = _prenorm(x_ref[0], g_ref[...], sh_ref[0], sc_ref[0]).astype(BF16)
    gd = cs_ref.shape[0]
    for g in range(FNET_GROUPS):
        r = _dot(h[:, g * gd:(g + 1) * gd], cs_ref[...])
        z_ref[0, 0, :, g * gd:(g + 1) * gd] = r[:, :gd].astype(BF16)
        z_ref[0, 1, :, g * gd:(g + 1) * gd] = r[:, gd:].astype(BF16)


def _fnet_feat(x, g_mix, shift, scale, cs, tm):
    b, t, d = x.shape
    return pl.pallas_call(
        _fnet_feat_kernel,
        grid=(b, t // tm),
        in_specs=[pl.BlockSpec((1, tm, d), lambda i, j: (i, j, 0)),
                  pl.BlockSpec((1, d), lambda i, j: (0, 0)),
                  pl.BlockSpec((1, 1, d), lambda i, j: (i, 0, 0)),
                  pl.BlockSpec((1, 1, d), lambda i, j: (i, 0, 0)),
                  pl.BlockSpec(cs.shape, lambda i, j: (0, 0))],
        out_specs=pl.BlockSpec((1, 2, tm, d), lambda i, j: (i, 0, j, 0)),
        out_shape=jax.ShapeDtypeStruct((b, 2, t, d), BF16),
        compiler_params=_cparams("parallel", "parallel"),
        name="fnet_feature_dft",
    )(x, g_mix.reshape(1, d), shift, scale, cs)


def _bmm_kernel(a_ref, z_ref, o_ref, acc_ref):
    k = pl.program_id(3)

    @pl.when(k == 0)
    def _():
        acc_ref[...] = jnp.zeros_like(acc_ref)

    acc_ref[...] += _dot(a_ref[...], z_ref[0])

    @pl.when(k == pl.num_programs(3) - 1)
    def _():
        o_ref[0] = acc_ref[...].astype(o_ref.dtype)


def _bmm(a, z, tm, tn, tk):
    m, kk = a.shape
    b, _, n = z.shape
    return pl.pallas_call(
        _bmm_kernel,
        grid=(b, m // tm, n // tn, kk // tk),
        in_specs=[pl.BlockSpec((tm, tk), lambda i, r, c, k: (r, k)),
                  pl.BlockSpec((1, tk, tn), lambda i, r, c, k: (i, k, c))],
        out_specs=pl.BlockSpec((1, tm, tn), lambda i, r, c, k: (i, r, c)),
        out_shape=jax.ShapeDtypeStruct((b, m, n), BF16),
        scratch_shapes=[pltpu.VMEM((tm, tn), F32)],
        compiler_params=_cparams("parallel", "parallel", "parallel", "arbitrary"),
        name="fnet_token_dft",
    )(a, z)


def _dft_cos_sin(n):
    lo = min(n, 128)
    hi = n // lo
    k = jnp.arange(n, dtype=jnp.int32)[:, None]
    w = 2.0 * math.pi / n
    ang_hi = ((k * (jnp.arange(hi, dtype=jnp.int32) * lo)[None, :]) % n).astype(F32) * w
    ang_lo = ((k * jnp.arange(lo, dtype=jnp.int32)[None, :]) % n).astype(F32) * w
    c1, s1 = jnp.cos(ang_hi)[:, :, None], jnp.sin(ang_hi)[:, :, None]
    c2, s2 = jnp.cos(ang_lo)[:, None, :], jnp.sin(ang_lo)[:, None, :]
    return (c1 * c2 - s1 * s2).reshape(n, n), (s1 * c2 + c1 * s2).reshape(n, n)


def _fnet_mixer(x, mods, g_mix, cs_feat, w_o_b):
    b, t, d = x.shape
    tm = min(512, t)
    z = _fnet_feat(x, g_mix, mods[0], mods[1], cs_feat, tm)
    ct, st = _dft_cos_sin(t)
    f = (jnp.concatenate([ct, -st], axis=1) * (t ** -0.5)).astype(BF16)
    mixed = _bmm(f, z.reshape(b, 2 * t, d), min(1024, t), d, 512)
    return _proj_residual(mixed, w_o_b, x, mods[2], tm)


def _router_kernel(x_ref, g_ref, sh_ref, sc_ref, wrT_ref, h_ref, aff_ref):
    h = _prenorm(x_ref[0], g_ref[...], sh_ref[0], sc_ref[0]).astype(BF16)
    h_ref[0] = h
    logits = _dot_nt(wrT_ref[...], h)
    m = jnp.max(logits, axis=0, keepdims=True)
    e = jnp.exp(logits - m)
    aff_ref[0] = e / jnp.sum(e, axis=0, keepdims=True)


def _router(x, g_ffn, shift, scale, w_router_t, tm):
    b, t, d = x.shape
    ne = w_router_t.shape[0]
    return pl.pallas_call(
        _router_kernel,
        grid=(b, t // tm),
        in_specs=[pl.BlockSpec((1, tm, d), lambda i, j: (i, j, 0)),
                  pl.BlockSpec((1, d), lambda i, j: (0, 0)),
                  pl.BlockSpec((1, 1, d), lambda i, j: (i, 0, 0)),
                  pl.BlockSpec((1, 1, d), lambda i, j: (i, 0, 0)),
                  pl.BlockSpec((ne, d), lambda i, j: (0, 0))],
        out_specs=[pl.BlockSpec((1, tm, d), lambda i, j: (i, j, 0)),
                   pl.BlockSpec((1, ne, tm), lambda i, j: (i, 0, j))],
        out_shape=[jax.ShapeDtypeStruct((b, t, d), BF16),
                   jax.ShapeDtypeStruct((b, ne, t), F32)],
        compiler_params=_cparams("parallel", "parallel"),
        name="moe_router",
    )(x, g_ffn.reshape(1, d), shift, scale, w_router_t)


def _expert_kernel(xe_ref, gate_ref, wg_ref, wu_ref, wd_ref, o_ref):
    xe = xe_ref[0]
    hid = _silu(_dot(xe, wg_ref[0])) * _dot(xe, wu_ref[0])
    o_ref[0] = _dot(hid.astype(BF16), wd_ref[0]) * gate_ref[0]


def _experts(xe, gate, wg, wu, wd, tr):
    ne, r, d = xe.shape
    f = wg.shape[-1]
    return pl.pallas_call(
        _expert_kernel,
        grid=(ne, r // tr),
        in_specs=[pl.BlockSpec((1, tr, d), lambda e, j: (e, j, 0)),
                  pl.BlockSpec((1, tr, 1), lambda e, j: (e, j, 0)),
                  pl.BlockSpec((1, d, f), lambda e, j: (e, 0, 0)),
                  pl.BlockSpec((1, d, f), lambda e, j: (e, 0, 0)),
                  pl.BlockSpec((1, f, d), lambda e, j: (e, 0, 0))],
        out_specs=pl.BlockSpec((1, tr, d), lambda e, j: (e, j, 0)),
        out_shape=jax.ShapeDtypeStruct((ne, r, d), F32),
        compiler_params=_cparams("parallel", "parallel"),
        name="moe_experts",
    )(xe, gate, wg, wu, wd)


def _ec_moe(x, mods, g_ffn, w_router_t, wg, wu, wd):
    b, t, d = x.shape
    ne = w_router_t.shape[0]
    cap = CAPACITY_FACTOR * t // ne
    h, aff_t = _router(x, g_ffn, mods[3], mods[4], w_router_t, min(512, t))
    gate, idx = lax.top_k(aff_t, cap)
    xe = jax.vmap(lambda hb, ib: hb[ib])(h, idx)
    xe = xe.transpose(1, 0, 2, 3).reshape(ne, b * cap, d)
    gate_e = gate.transpose(1, 0, 2).reshape(ne, b * cap, 1)
    ye = _experts(xe, gate_e, wg, wu, wd, min(1024, b * cap))
    ye = ye.reshape(ne, b, cap, d).transpose(1, 0, 2, 3)
    out = jax.vmap(lambda ib, yb: jnp.zeros((t, d), F32).at[ib.reshape(-1)].add(yb.reshape(-1, d)))(idx, ye)
    return x + mods[5] * out


def kernel(x, c, ctx, c_ctx, w_mod, b_mod, g_mix, g_ffn, mla_w_in, mla_g_q, mla_w_uq, mla_g_kv, mla_w_ukv,
           mla_g_qn, mla_g_kn, mla_w_o, ssd_w_in, ssd_conv_w, ssd_conv_b, ssd_dt_bias, ssd_a_log, ssd_d,
           ssd_g_norm, ssd_w_out, fnet_w_o, moe_w_router, moe_w_gate, moe_w_up, moe_w_down):
    depth = w_mod.shape[0]
    b, t, d = x.shape
    cx = ctx
    rows = 16
    cc = jnp.concatenate([c, c_ctx[None, :], jnp.zeros((rows - b - 1, d), F32)], axis=0)
    mods = _mods(cc, w_mod, b_mod)
    for i in range(depth):
        kind, j = i % 3, i // 3
        last = i == depth - 1
        ml = mods[i, :b].reshape(b, 1, N_MOD, d)
        mods_l = [ml[:, :, n] for n in range(N_MOD)]
        mc = jnp.broadcast_to(mods[i, b].reshape(1, 1, N_MOD, d), (b, 1, N_MOD, d))
        mods_c = [mc[:, :, n] for n in range(N_MOD)]
        if kind == 0:
            x, cx_new = _mla_mixer(x, cx, i, mods_l, mods_c, g_mix[i], mla_w_in[j], mla_g_q[j], mla_w_uq[j],
                                   mla_g_kv[j], mla_w_ukv[j], mla_g_qn[j], mla_g_kn[j], mla_w_o[j], not last)
        elif kind == 1:
            x, cx_new = _ssd_mixer(x, cx, mods_l, mods_c, g_mix[i], ssd_w_in[j], ssd_conv_w[j], ssd_conv_b[j],
                                   ssd_dt_bias[j], ssd_a_log[j], ssd_d[j], ssd_g_norm[j], ssd_w_out[j], not last)
        else:
            gd = d // FNET_GROUPS
            cd, sd = _dft_cos_sin(gd)
            cs_feat = (jnp.concatenate([cd, sd], axis=1) * (gd ** -0.5)).astype(BF16)
            w_o_b = fnet_w_o[j].astype(BF16)
            x_new = _fnet_mixer(x, mods_l, g_mix[i], cs_feat, w_o_b)
            cx_new = _fnet_mixer(cx, mods_c, g_mix[i], cs_feat, w_o_b) if not last else None
            x = x_new
        w_r_t = moe_w_router[i].T.astype(BF16)
        wg, wu, wd = moe_w_gate[i].astype(BF16), moe_w_up[i].astype(BF16), moe_w_down[i].astype(BF16)
        x = _ec_moe(x, mods_l, g_ffn[i], w_r_t, wg, wu, wd)
        if not last:
            cx = _ec_moe(cx_new, mods_c, g_ffn[i], w_r_t, wg, wu, wd)
    return x
```
